```python
import jax, jax.numpy as jnp
from jax import lax
import numpy as np

D_MODEL = 1024
BATCH = 2
SEQ = 8192
DEPTH = 2

HEAD_DIM = 64
GRID_W = 64
ROPE_THETA = 10000.0
NORM_EPS = 1e-6
ATTN_BLOCK = 128
A_Q_HEADS = 8
A_KV_HEADS = 2
B_Q_HEADS = 8
B_KV_HEADS = 2
B_HALF_WINDOW = 128
AB_IN = (A_Q_HEADS + 2 * A_KV_HEADS + B_Q_HEADS + 2 * B_KV_HEADS) * HEAD_DIM
AB_OUT = (A_Q_HEADS + B_Q_HEADS) * HEAD_DIM
C_PATTERNS = ((128, 1), (512, 4), (2048, 16))
C_Q_HEADS = 8
C_KV_HEADS = 2
C_GROUP_IN = (C_Q_HEADS + 2 * C_KV_HEADS) * HEAD_DIM
C_IN = len(C_PATTERNS) * C_GROUP_IN
C_OUT = C_Q_HEADS * HEAD_DIM
N_EXPERTS = 32
TOP_K = 4
D_EXPERT = 1024
SWIGLU_LIMIT = 7.0
SWIGLU_ALPHA = 1.702
MOE_BLOCK = 128
PLE_DIM = 256
N_EVEN = (DEPTH + 1) // 2
N_ODD = DEPTH // 2

kernel_name = "hybrid_axial_window_dilated_moe_encoder"


def rms_norm(x, gain):
    xf = x.astype(jnp.float32)
    y = xf * lax.rsqrt(jnp.mean(xf * xf, axis=-1, keepdims=True) + NORM_EPS)
    return (y * gain.astype(jnp.float32)).astype(x.dtype)


def rope_table(pos, dim):
    freqs = ROPE_THETA ** (-jnp.arange(0, dim, 2, dtype=jnp.float32) / dim)
    ang = pos.astype(jnp.float32)[:, None] * freqs[None, :]
    return jnp.cos(ang), jnp.sin(ang)


def apply_rope(x, cos, sin):
    xf = x.astype(jnp.float32)
    x1, x2 = jnp.split(xf, 2, axis=-1)
    c = cos[None, :, None, :]
    s = sin[None, :, None, :]
    return jnp.concatenate([x1 * c - x2 * s, x2 * c + x1 * s], axis=-1).astype(x.dtype)


def apply_axial_rope(x, rope_row, rope_col):
    half = HEAD_DIM // 2
    return jnp.concatenate([apply_rope(x[..., :half], *rope_row),
                            apply_rope(x[..., half:], *rope_col)], axis=-1)


def dense_attention(q, k, v):
    b, s, hq, d = q.shape
    hkv = k.shape[2]
    g = hq // hkv
    nblk = s // ATTN_BLOCK
    scale = d ** -0.5
    qb = q.reshape(b, nblk, ATTN_BLOCK, hkv, g, d).transpose(1, 0, 2, 3, 4, 5)

    def one_block(qi):
        sc = jnp.einsum('bqhgd,bkhd->bhgqk', qi, k, preferred_element_type=jnp.float32) * scale
        pr = jax.nn.softmax(sc, axis=-1)
        return jnp.einsum('bhgqk,bkhd->bqhgd', pr.astype(v.dtype), v)

    o = lax.map(one_block, qb)
    return o.transpose(1, 0, 2, 3, 4, 5).reshape(b, s, hq, d)


def banded_attention(q, k, v, half_window, sink=None):
    b, l, hq, d = q.shape
    hkv = k.shape[2]
    g = hq // hkv
    blk = half_window
    nb = -(-l // blk)
    lp = nb * blk
    scale = d ** -0.5
    qp = jnp.pad(q, ((0, 0), (0, lp - l), (0, 0), (0, 0))).reshape(b, nb, blk, hkv, g, d)
    pad_kv = ((0, 0), (blk, lp - l + blk), (0, 0), (0, 0))
    kp = jnp.pad(k, pad_kv).reshape(b, nb + 2, blk, hkv, d)
    vp = jnp.pad(v, pad_kv).reshape(b, nb + 2, blk, hkv, d)
    kb = jnp.concatenate([kp[:, :-2], kp[:, 1:-1], kp[:, 2:]], axis=2)
    vb = jnp.concatenate([vp[:, :-2], vp[:, 1:-1], vp[:, 2:]], axis=2)
    sc = jnp.einsum('bnqhgd,bnkhd->bnhgqk', qp, kb, preferred_element_type=jnp.float32) * scale
    qpos = jnp.arange(nb)[:, None] * blk + jnp.arange(blk)[None, :]
    kpos = (jnp.arange(nb)[:, None] - 1) * blk + jnp.arange(3 * blk)[None, :]
    valid = ((jnp.abs(qpos[:, :, None] - kpos[:, None, :]) <= half_window)
             & (kpos[:, None, :] >= 0) & (kpos[:, None, :] < l))
    sc = jnp.where(valid[None, :, None, None], sc, -jnp.inf)
    m = jnp.max(sc, axis=-1, keepdims=True)
    if sink is not None:
        sink_b = sink.astype(jnp.float32).reshape(hkv, g)[None, None, :, :, None, None]
        m = jnp.maximum(m, sink_b)
    e = jnp.exp(sc - m)
    den = jnp.sum(e, axis=-1, keepdims=True)
    if sink is not None:
        den = den + jnp.exp(sink_b - m)
    o = jnp.einsum('bnhgqk,bnkhd->bnqhgd', (e / den).astype(v.dtype), vb)
    lse = (m + jnp.log(den))[..., 0]
    o = o.reshape(b, lp, hq, d)[:, :l]
    lse = lse.transpose(0, 1, 4, 2, 3).reshape(b, lp, hq)[:, :l]
    return o, lse


def dilated_attention(q, k, v, window, dilation):
    b, s, hq, d = q.shape
    n_side = window // (2 * dilation)
    l = s // dilation

    def to_classes(z):
        h = z.shape[2]
        return z.reshape(b, l, dilation, h, d).transpose(0, 2, 1, 3, 4).reshape(b * dilation, l, h, d)

    o, lse = banded_attention(to_classes(q), to_classes(k), to_classes(v), n_side)
    o = o.reshape(b, dilation, l, hq, d).transpose(0, 2, 1, 3, 4).reshape(b, s, hq, d)
    lse = lse.reshape(b, dilation, l, hq).transpose(0, 2, 1, 3).reshape(b, s, hq)
    return o, lse


def mixer_ab(hn, w_in, q_gain, k_gain, sink, w_out, rope_1d, rope_row, rope_col):
    b, s, _ = hn.shape
    proj = hn @ w_in
    widths = [A_Q_HEADS, A_KV_HEADS, A_KV_HEADS, B_Q_HEADS, B_KV_HEADS, B_KV_HEADS]
    cuts = [int(c) * HEAD_DIM for c in np.cumsum(widths)[:-1]]
    qa, ka, va, qb, kb, vb = [z.reshape(b, s, -1, HEAD_DIM) for z in jnp.split(proj, cuts, axis=-1)]
    qa = apply_axial_rope(rms_norm(qa, q_gain), rope_row, rope_col)
    ka = apply_axial_rope(rms_norm(ka, k_gain), rope_row, rope_col)
    oa = dense_attention(qa, ka, va)
    qb = apply_rope(qb, *rope_1d)
    kb = apply_rope(kb, *rope_1d)
    ob, _ = banded_attention(qb, kb, vb, B_HALF_WINDOW, sink)
    o = jnp.concatenate([oa.reshape(b, s, -1), ob.reshape(b, s, -1)], axis=-1)
    return o @ w_out


def mixer_c(hn, w_in, w_out, rope_1d):
    b, s, _ = hn.shape
    proj = hn @ w_in
    cuts = [C_Q_HEADS * HEAD_DIM, (C_Q_HEADS + C_KV_HEADS) * HEAD_DIM]
    outs, lses = [], []
    for gi, (window, dilation) in enumerate(C_PATTERNS):
        grp = proj[..., gi * C_GROUP_IN:(gi + 1) * C_GROUP_IN]
        q, k, v = [z.reshape(b, s, -1, HEAD_DIM) for z in jnp.split(grp, cuts, axis=-1)]
        q = apply_rope(q, *rope_1d)
        k = apply_rope(k, *rope_1d)
        o, lse = dilated_attention(q, k, v, window, dilation)
        outs.append(o)
        lses.append(lse)
    wts = jax.nn.softmax(jnp.stack(lses, axis=0), axis=0)
    o = jnp.einsum('gbsh,gbshd->bshd', wts.astype(outs[0].dtype), jnp.stack(outs, axis=0))
    return o.reshape(b, s, C_OUT) @ w_out


def moe(hn, router_w, router_b, w_up, b_up, w_down, b_down):
    b, s, dm = hn.shape
    t = b * s
    x = hn.reshape(t, dm)
    logits = (x @ router_w + router_b).astype(jnp.float32)
    top_val, top_idx = lax.top_k(logits, TOP_K)
    gates = jax.nn.softmax(top_val, axis=-1)
    a = t * TOP_K
    flat_e = top_idx.reshape(a).astype(jnp.int32)
    flat_tok = jnp.repeat(jnp.arange(t, dtype=jnp.int32), TOP_K)
    flat_gate = gates.reshape(a)
    order = jnp.argsort(flat_e)
    e_sorted = flat_e[order]
    tok_sorted = flat_tok[order]
    gate_sorted = flat_gate[order]
    counts = jnp.bincount(flat_e, length=N_EXPERTS)
    starts = jnp.cumsum(counts) - counts
    padded = (counts + MOE_BLOCK - 1) // MOE_BLOCK * MOE_BLOCK
    pends = jnp.cumsum(padded)
    pstarts = pends - padded
    dest = pstarts[e_sorted] + jnp.arange(a, dtype=jnp.int32) - starts[e_sorted]
    n_blocks = -(-a // MOE_BLOCK) + N_EXPERTS
    n_slots = n_blocks * MOE_BLOCK
    slot_tok = jnp.full((n_slots,), t, jnp.int32).at[dest].set(tok_sorted)
    slot_gate = jnp.zeros((n_slots,), jnp.float32).at[dest].set(gate_sorted)
    block_e = jnp.minimum(jnp.searchsorted(pends, jnp.arange(n_blocks) * MOE_BLOCK, side='right'),
                          N_EXPERTS - 1)
    x_pad = jnp.concatenate([x, jnp.zeros((1, dm), x.dtype)], axis=0)
    xb = x_pad[slot_tok].reshape(n_blocks, MOE_BLOCK, dm)

    def expert_block(args):
        xi, e = args
        hu = xi @ w_up[e] + b_up[e]
        glu = jnp.minimum(hu[:, ::2], SWIGLU_LIMIT)
        lin = jnp.clip(hu[:, 1::2], -SWIGLU_LIMIT, SWIGLU_LIMIT)
        act = glu * jax.nn.sigmoid(SWIGLU_ALPHA * glu) * (lin + 1.0)
        return act @ w_down[e] + b_down[e]

    yb = lax.map(expert_block, (xb, block_e))
    y = yb.reshape(n_slots, dm) * slot_gate[:, None].astype(yb.dtype)
    out = jnp.zeros((t + 1, dm), yb.dtype).at[slot_tok].add(y)[:t]
    return out.reshape(b, s, dm)


def setup_inputs(seed: int = 0) -> dict:
    key = jax.random.key(seed)
    ks = jax.random.split(key, 24)
    f32 = jnp.float32

    def nrm(k, shape, fan_in):
        return jax.random.normal(k, shape, f32) * fan_in ** -0.5

    def gain(k, shape):
        return 1.0 + 0.05 * jax.random.normal(k, shape, f32)

    return {
        "x": jax.random.normal(ks[0], (BATCH, SEQ, D_MODEL), f32),
        "p": jax.random.normal(ks[1], (DEPTH, BATCH, SEQ, PLE_DIM), f32),
        "norm_mix": gain(ks[2], (DEPTH, D_MODEL)),
        "ab_w_in": nrm(ks[3], (N_EVEN, D_MODEL, AB_IN), D_MODEL),
        "ab_q_norm": gain(ks[4], (N_EVEN, HEAD_DIM)),
        "ab_k_norm": gain(ks[5], (N_EVEN, HEAD_DIM)),
        "ab_sink": 0.5 * jax.random.normal(ks[6], (N_EVEN, B_Q_HEADS), f32),
        "ab_w_out": nrm(ks[7], (N_EVEN, AB_OUT, D_MODEL), AB_OUT),
        "c_w_in": nrm(ks[8], (N_ODD, D_MODEL, C_IN), D_MODEL),
        "c_w_out": nrm(ks[9], (N_ODD, C_OUT, D_MODEL), C_OUT),
        "norm_ffn": gain(ks[10], (DEPTH, D_MODEL)),
        "router_w": nrm(ks[11], (DEPTH, D_MODEL, N_EXPERTS), D_MODEL),
        "router_b": 0.01 * jax.random.normal(ks[12], (DEPTH, N_EXPERTS), f32),
        "exp_w_up": nrm(ks[13], (DEPTH, N_EXPERTS, D_MODEL, 2 * D_EXPERT), D_MODEL),
        "exp_b_up": 0.01 * jax.random.normal(ks[14], (DEPTH, N_EXPERTS, 2 * D_EXPERT), f32),
        "exp_w_down": nrm(ks[15], (DEPTH, N_EXPERTS, D_EXPERT, D_MODEL), D_EXPERT),
        "exp_b_down": 0.01 * jax.random.normal(ks[16], (DEPTH, N_EXPERTS, D_MODEL), f32),
        "ple_w_proj": nrm(ks[17], (DEPTH, PLE_DIM, D_MODEL), PLE_DIM),
        "ple_w_gate": nrm(ks[18], (DEPTH, D_MODEL, D_MODEL), D_MODEL),
        "norm_final": gain(ks[19], (D_MODEL,)),
    }


def reference(x, p, norm_mix, ab_w_in, ab_q_norm, ab_k_norm, ab_sink, ab_w_out, c_w_in, c_w_out,
              norm_ffn, router_w, router_b, exp_w_up, exp_b_up, exp_w_down, exp_b_down,
              ple_w_proj, ple_w_gate, norm_final):
    b, s, _ = x.shape
    rows = s // GRID_W
    pos = jnp.arange(s)
    row_idx = jnp.repeat(jnp.arange(rows), GRID_W)
    col_idx = jnp.tile(jnp.arange(GRID_W), rows)
    rope_1d = rope_table(pos, HEAD_DIM)
    rope_row = rope_table(row_idx, HEAD_DIM // 2)
    rope_col = rope_table(col_idx, HEAD_DIM // 2)
    h = x
    for i in range(DEPTH):
        j = i // 2
        hn = rms_norm(h, norm_mix[i])
        if i % 2 == 0:
            mix = mixer_ab(hn, ab_w_in[j], ab_q_norm[j], ab_k_norm[j], ab_sink[j], ab_w_out[j],
                           rope_1d, rope_row, rope_col)
        else:
            mix = mixer_c(hn, c_w_in[j], c_w_out[j], rope_1d)
        h = h + mix
        h = h + moe(rms_norm(h, norm_ffn[i]), router_w[i], router_b[i], exp_w_up[i], exp_b_up[i],
                    exp_w_down[i], exp_b_down[i])
        gate = jax.nn.sigmoid(h @ ple_w_gate[i])
        h = h + (p[i] @ ple_w_proj[i]) * gate
    return rms_norm(h, norm_final)
```

```python
import functools

import numpy as np
import jax
import jax.numpy as jnp
from jax import lax
from jax.experimental import pallas as pl
from jax.experimental.pallas import tpu as pltpu

F32 = jnp.float32
BF16 = jnp.bfloat16

D_MODEL = 1024
HEAD_DIM = 64
GRID_W = 64
ROPE_THETA = 10000.0
NORM_EPS = 1e-6
Q_HEADS = 8
KV_HEADS = 2
QW = Q_HEADS * HEAD_DIM
KVW = KV_HEADS * HEAD_DIM
UNIT_IN = QW + 2 * KVW
B_HALF_WINDOW = 128
C_PATTERNS = ((128, 1), (512, 4), (2048, 16))
N_EXPERTS = 32
TOP_K = 4
D_EXPERT = 1024
SWIGLU_LIMIT = 7.0
SWIGLU_ALPHA = 1.702
PLE_DIM = 256
LANES = 128

VMEM_LIMIT = 56 * 1024 * 1024

PROJ_TM = 256
ATT_TQ = 512
ATT_TK = 512
BAND_TQ = 128
OUT_TM = 256
ROUTER_TM = 256
MOE_BM = 256
PLE_TM = 256


def _nt(a, b):
    return lax.dot_general(a, b, (((1,), (1,)), ((), ())), preferred_element_type=F32)


def _cparams(sem):
    return pltpu.CompilerParams(dimension_semantics=sem, vmem_limit_bytes=VMEM_LIMIT)


def _rope_slab(y, cos, sin_signed, lane, shift):
    first = (lane % (2 * shift)) < shift
    rot = jnp.where(first, pltpu.roll(y, LANES - shift, 1), pltpu.roll(y, shift, 1))
    return y * cos + rot * sin_signed


def _head_rms(a, gain, bd):
    a2 = a * a
    hi = a2.astype(BF16)
    lo = (a2 - hi.astype(F32)).astype(BF16)
    ss = jnp.dot(hi, bd, preferred_element_type=F32) + jnp.dot(lo, bd, preferred_element_type=F32)
    return a * lax.rsqrt(ss * (1.0 / HEAD_DIM) + NORM_EPS) * gain


def _store_kv4(ref, y, lane):
    low = lane < HEAD_DIM
    lo0 = jnp.where(low, y, 0.0)
    hi1 = jnp.where(low, 0.0, y)
    hi0 = pltpu.roll(lo0, HEAD_DIM, 1)
    lo1 = pltpu.roll(hi1, HEAD_DIM, 1)
    ref[:, 0 * LANES:1 * LANES] = lo0.astype(BF16)
    ref[:, 1 * LANES:2 * LANES] = hi0.astype(BF16)
    ref[:, 2 * LANES:3 * LANES] = lo1.astype(BF16)
    ref[:, 3 * LANES:4 * LANES] = hi1.astype(BF16)


def _proj_kernel(x_ref, g_ref, w_ref, cos1_ref, sin1_ref, cosa_ref, sina_ref, qg_ref, kg_ref, bd_ref,
                 *out_refs, axial_units):
    x = x_ref[...]
    r = lax.rsqrt(jnp.mean(x * x, axis=-1, keepdims=True) + NORM_EPS)
    hn = (x * r * g_ref[...]).astype(BF16)
    tm = x.shape[0]
    lane = lax.broadcasted_iota(jnp.int32, (tm, LANES), 1)
    n_units = len(out_refs) // 3
    scale = HEAD_DIM ** -0.5
    for u in range(n_units):
        axial = u in axial_units
        q_ref, k_ref, v_ref = out_refs[3 * u:3 * u + 3]
        c0 = u * UNIT_IN
        if axial:
            cos, sin, shift = cosa_ref[...], sina_ref[...], HEAD_DIM // 4
        else:
            cos, sin, shift = cos1_ref[...], sin1_ref[...], HEAD_DIM // 2
        accq = jnp.dot(hn, w_ref[:, c0:c0 + QW], preferred_element_type=F32)
        for j in range(QW // LANES):
            a = accq[:, j * LANES:(j + 1) * LANES]
            if axial:
                a = _head_rms(a, qg_ref[...], bd_ref[...])
            a = _rope_slab(a, cos, sin, lane, shift) * scale
            q_ref[:, j * LANES:(j + 1) * LANES] = a.astype(BF16)
        acckv = jnp.dot(hn, w_ref[:, c0 + QW:c0 + UNIT_IN], preferred_element_type=F32)
        a = acckv[:, 0:LANES]
        if axial:
            a = _head_rms(a, kg_ref[...], bd_ref[...])
        a = _rope_slab(a, cos, sin, lane, shift)
        _store_kv4(k_ref, a, lane)
        _store_kv4(v_ref, acckv[:, LANES:2 * LANES], lane)


def _project(h2d, gain, w_bf16, tabs, qg, kg, bd, axial_units, seq):
    t = h2d.shape[0]
    n = w_bf16.shape[1]
    n_units = n // UNIT_IN
    tm = PROJ_TM
    nt_seq = seq // tm
    cos1, sin1, cosa, sina = tabs
    row = lambda i: (i, 0)
    pos = lambda i: (i % nt_seq, 0)
    const = lambda i: (0, 0)
    out_shape = [jax.ShapeDtypeStruct((t, QW), BF16)] * (3 * n_units)
    out_specs = [pl.BlockSpec((tm, QW), row)] * (3 * n_units)
    return pl.pallas_call(
        functools.partial(_proj_kernel, axial_units=axial_units),
        grid=(t // tm,),
        in_specs=[
            pl.BlockSpec((tm, D_MODEL), row),
            pl.BlockSpec((1, D_MODEL), const),
            pl.BlockSpec((D_MODEL, n), const),
            pl.BlockSpec((tm, LANES), pos),
            pl.BlockSpec((tm, LANES), pos),
            pl.BlockSpec((tm, LANES), pos),
            pl.BlockSpec((tm, LANES), pos),
            pl.BlockSpec((1, LANES), const),
            pl.BlockSpec((1, LANES), const),
            pl.BlockSpec((LANES, LANES), const),
        ],
        out_specs=out_specs,
        out_shape=out_shape,
        compiler_params=_cparams(("parallel",)),
        name="norm_proj_rope",
    )(h2d, gain.reshape(1, D_MODEL), w_bf16, cos1, sin1, cosa, sina, qg, kg, bd)


def _dense_attn_kernel(q_ref, k_ref, v_ref, o_ref, *, seq, tk):
    q = q_ref[...]
    tq = q.shape[0]
    out = jnp.zeros((tq, LANES), F32)
    for half in range(2):
        c = half * LANES

        def body(j, carry, c=c):
            m, l, acc = carry
            start = pl.multiple_of(j * tk, tk)
            k = k_ref[pl.ds(start, tk), c:c + LANES]
            v = v_ref[pl.ds(start, tk), c:c + LANES]
            s = _nt(q, k)
            m_new = jnp.maximum(m, jnp.max(s, axis=1, keepdims=True))
            p = jnp.exp(s - m_new)
            alpha = jnp.exp(m - m_new)
            l = alpha * l + jnp.sum(p, axis=1, keepdims=True)
            acc = alpha * acc + jnp.dot(p.astype(BF16), v, preferred_element_type=F32)
            return m_new, l, acc

        init = (jnp.full((tq, 1), -jnp.inf, F32), jnp.zeros((tq, 1), F32), jnp.zeros((tq, LANES), F32))
        _, l, acc = lax.fori_loop(0, seq // tk, body, init)
        out = out + acc / l
    o_ref[...] = out.astype(BF16)


def _dense_attention(q, k4, v4, batch, seq):
    q3 = q.reshape(batch, seq, QW)
    k3 = k4.reshape(batch, seq, 4 * LANES)
    v3 = v4.reshape(batch, seq, 4 * LANES)
    tq = ATT_TQ
    kv_spec = pl.BlockSpec((None, seq, 2 * LANES), lambda b, i, p: (b, 0, p // 2))
    out = pl.pallas_call(
        functools.partial(_dense_attn_kernel, seq=seq, tk=ATT_TK),
        grid=(batch, seq // tq, QW // LANES),
        in_specs=[pl.BlockSpec((None, tq, LANES), lambda b, i, p: (b, i, p)), kv_spec, kv_spec],
        out_specs=pl.BlockSpec((None, tq, LANES), lambda b, i, p: (b, i, p)),
        out_shape=jax.ShapeDtypeStruct((batch, seq, QW), BF16),
        compiler_params=_cparams(("parallel", "parallel", "arbitrary")),
        name="dense_attention",
    )(q3, k3, v3)
    return out.reshape(batch * seq, QW)


def _banded_kernel(*refs, tq, length, half_window, use_sink, want_lse):
    if use_sink:
        sink_ref, refs = refs[0], refs[1:]
    q_ref, kp_ref, ko_ref, kn_ref, vp_ref, vo_ref, vn_ref = refs[:7]
    o_ref = refs[7]
    lse_ref = refs[8] if want_lse else None
    i = pl.program_id(2)
    qpos = i * tq + lax.broadcasted_iota(jnp.int32, (tq, 3 * tq), 0)
    kpos = (i - 1) * tq + lax.broadcasted_iota(jnp.int32, (tq, 3 * tq), 1)
    valid = (jnp.abs(qpos - kpos) <= half_window) & (kpos >= 0) & (kpos < length)
    lane = lax.broadcasted_iota(jnp.int32, (tq, LANES), 1)
    low = lane < HEAD_DIM
    for pair in range(QW // (2 * HEAD_DIM)):
        q = q_ref[:, pair * LANES:(pair + 1) * LANES]
        hk = pair // 2
        acc = jnp.zeros((tq, LANES), F32)
        lse_slab = jnp.zeros((tq, LANES), F32)
        for half in range(2):
            c = (hk * 2 + half) * LANES
            kc = jnp.concatenate([kp_ref[:, c:c + LANES], ko_ref[:, c:c + LANES], kn_ref[:, c:c + LANES]], axis=0)
            vc = jnp.concatenate([vp_ref[:, c:c + LANES], vo_ref[:, c:c + LANES], vn_ref[:, c:c + LANES]], axis=0)
            s = jnp.where(valid, _nt(q, kc), -jnp.inf)
            m = jnp.max(s, axis=1, keepdims=True)
            if use_sink:
                sk = sink_ref[2 * pair + half]
                m = jnp.maximum(m, sk)
            e = jnp.exp(s - m)
            den = jnp.sum(e, axis=1, keepdims=True)
            if use_sink:
                den = den + jnp.exp(sk - m)
            acc = acc + jnp.dot((e / den).astype(BF16), vc, preferred_element_type=F32)
            if want_lse:
                lse = m + jnp.log(den)
                lse_slab = jnp.where(low == (half == 0), lse, lse_slab)
        o_ref[:, pair * LANES:(pair + 1) * LANES] = acc.astype(o_ref.dtype)
        if want_lse:
            lse_ref[:, pair * LANES:(pair + 1) * LANES] = lse_slab


def _banded_attention(q, k4, v4, batch, seq, half_window, dilation, sink=None, want_lse=False, out_dtype=BF16):
    length = seq // dilation
    tq = BAND_TQ
    nb = length // tq
    width = dilation * QW
    q3 = q.reshape(batch, length, width)
    k3 = k4.reshape(batch, length, width)
    v3 = v4.reshape(batch, length, width)
    own = lambda b, r, i, *_: (b, i, r)
    prev = lambda b, r, i, *_: (b, jnp.maximum(i - 1, 0), r)
    nxt = lambda b, r, i, *_: (b, jnp.minimum(i + 1, nb - 1), r)
    blk = lambda f: pl.BlockSpec((None, tq, QW), f)
    in_specs = [blk(own), blk(prev), blk(own), blk(nxt), blk(prev), blk(own), blk(nxt)]
    args = [q3, k3, k3, k3, v3, v3, v3]
    if sink is not None:
        in_specs = [pl.BlockSpec(memory_space=pltpu.SMEM)] + in_specs
        args = [sink.astype(F32)] + args
    out_shape = [jax.ShapeDtypeStruct((batch, length, width), out_dtype)]
    out_specs = [blk(own)]
    if want_lse:
        out_shape.append(jax.ShapeDtypeStruct((batch, length, width), F32))
        out_specs.append(blk(own))
    outs = pl.pallas_call(
        functools.partial(_banded_kernel, tq=tq, length=length, half_window=half_window,
                          use_sink=sink is not None, want_lse=want_lse),
        grid=(batch, dilation, nb),
        in_specs=in_specs,
        out_specs=out_specs,
        out_shape=out_shape,
        compiler_params=_cparams(("parallel", "parallel", "parallel")),
        name="banded_attention",
    )(*args)
    return [o.reshape(batch * seq, QW) for o in outs]


def _out_ab_kernel(h_ref, oa_ref, ob_ref, wa_ref, wb_ref, o_ref):
    mix = jnp.dot(oa_ref[...], wa_ref[...], preferred_element_type=F32)
    mix = mix + jnp.dot(ob_ref[...], wb_ref[...], preferred_element_type=F32)
    o_ref[...] = h_ref[...] + mix


def _out_proj_ab(h2d, oa, ob, w_out_bf16):
    t = h2d.shape[0]
    tm = OUT_TM
    row = lambda i: (i, 0)
    const = lambda i: (0, 0)
    return pl.pallas_call(
        _out_ab_kernel,
        grid=(t // tm,),
        in_specs=[pl.BlockSpec((tm, D_MODEL), row), pl.BlockSpec((tm, QW), row), pl.BlockSpec((tm, QW), row),
                  pl.BlockSpec((QW, D_MODEL), const), pl.BlockSpec((QW, D_MODEL), const)],
        out_specs=pl.BlockSpec((tm, D_MODEL), row),
        out_shape=jax.ShapeDtypeStruct((t, D_MODEL), F32),
        compiler_params=_cparams(("parallel",)),
        name="out_proj_ab",
    )(h2d, oa, ob, w_out_bf16[:QW], w_out_bf16[QW:])


def _out_c_kernel(h_ref, o0_ref, o1_ref, o2_ref, l0_ref, l1_ref, l2_ref, w_ref, o_ref):
    l0, l1, l2 = l0_ref[...], l1_ref[...], l2_ref[...]
    m = jnp.maximum(jnp.maximum(l0, l1), l2)
    e0, e1, e2 = jnp.exp(l0 - m), jnp.exp(l1 - m), jnp.exp(l2 - m)
    inv = 1.0 / (e0 + e1 + e2)
    merged = (e0 * inv) * o0_ref[...] + (e1 * inv) * o1_ref[...] + (e2 * inv) * o2_ref[...]
    o_ref[...] = h_ref[...] + jnp.dot(merged.astype(BF16), w_ref[...], preferred_element_type=F32)


def _out_proj_c(h2d, outs, lses, w_out_bf16):
    t = h2d.shape[0]
    tm = OUT_TM
    row = lambda i: (i, 0)
    const = lambda i: (0, 0)
    half = pl.BlockSpec((tm, QW), row)
    return pl.pallas_call(
        _out_c_kernel,
        grid=(t // tm,),
        in_specs=[pl.BlockSpec((tm, D_MODEL), row)] + [half] * 6 + [pl.BlockSpec((QW, D_MODEL), const)],
        out_specs=pl.BlockSpec((tm, D_MODEL), row),
        out_shape=jax.ShapeDtypeStruct((t, D_MODEL), F32),
        compiler_params=_cparams(("parallel",)),
        name="out_proj_c",
    )(h2d, *outs, *lses, w_out_bf16)


def _router_kernel(x_ref, g_ref, rw_ref, rb_ref, hn_ref, idx_ref, gate_ref):
    x = x_ref[...]
    r = lax.rsqrt(jnp.mean(x * x, axis=-1, keepdims=True) + NORM_EPS)
    hn = (x * r * g_ref[...]).astype(BF16)
    hn_ref[...] = hn
    logits = jnp.dot(hn, rw_ref[...], preferred_element_type=F32) + rb_ref[...]
    tm = x.shape[0]
    lane = lax.broadcasted_iota(jnp.int32, (tm, LANES), 1)
    idx_out = jnp.zeros((tm, LANES), jnp.int32)
    val_out = jnp.zeros((tm, LANES), F32)
    top = None
    den = jnp.zeros((tm, 1), F32)
    for kk in range(TOP_K):
        mx = jnp.max(logits, axis=1, keepdims=True)
        sel = jnp.min(jnp.where(logits == mx, lane, LANES), axis=1, keepdims=True)
        if top is None:
            top = mx
        ev = jnp.exp(mx - top)
        den = den + ev
        idx_out = jnp.where(lane == kk, sel, idx_out)
        val_out = jnp.where(lane == kk, ev, val_out)
        logits = jnp.where(lane == sel, -jnp.inf, logits)
    idx_ref[...] = idx_out
    gate_ref[...] = val_out / den


def _router(h2d, gain, rw_pad, rb_pad):
    t = h2d.shape[0]
    tm = ROUTER_TM
    row = lambda i: (i, 0)
    const = lambda i: (0, 0)
    return pl.pallas_call(
        _router_kernel,
        grid=(t // tm,),
        in_specs=[pl.BlockSpec((tm, D_MODEL), row), pl.BlockSpec((1, D_MODEL), const),
                  pl.BlockSpec((D_MODEL, LANES), const), pl.BlockSpec((1, LANES), const)],
        out_specs=[pl.BlockSpec((tm, D_MODEL), row), pl.BlockSpec((tm, LANES), row), pl.BlockSpec((tm, LANES), row)],
        out_shape=[jax.ShapeDtypeStruct((t, D_MODEL), BF16), jax.ShapeDtypeStruct((t, LANES), jnp.int32),
                   jax.ShapeDtypeStruct((t, LANES), F32)],
        compiler_params=_cparams(("parallel",)),
        name="moe_router",
    )(h2d, gain.reshape(1, D_MODEL), rw_pad, rb_pad)


def _expert_kernel(be_ref, nu_ref, x_ref, gate_ref, wg_ref, wl_ref, bg_ref, bl_ref, wd_ref, bd_ref, y_ref):
    i = pl.program_id(0)

    @pl.when(i < nu_ref[0])
    def _():
        x = x_ref[...]
        g = jnp.dot(x, wg_ref[...], preferred_element_type=F32) + bg_ref[...]
        l = jnp.dot(x, wl_ref[...], preferred_element_type=F32) + bl_ref[...]
        g = jnp.minimum(g, SWIGLU_LIMIT)
        l = jnp.clip(l, -SWIGLU_LIMIT, SWIGLU_LIMIT)
        act = g * (1.0 / (1.0 + jnp.exp(-SWIGLU_ALPHA * g))) * (l + 1.0)
        y = jnp.dot(act.astype(BF16), wd_ref[...], preferred_element_type=F32) + bd_ref[...]
        y_ref[...] = y * gate_ref[...]

    @pl.when(i >= nu_ref[0])
    def _():
        y_ref[...] = jnp.zeros_like(y_ref)


def _experts(xs, slot_gate, block_e, n_used, wg, wl, bg, bl, wd, bd):
    n_slots = xs.shape[0]
    bm = MOE_BM
    n_blocks = n_slots // bm
    row = lambda i, be, nu: (i, 0)
    wsel = lambda i, be, nu: (be[i], 0, 0)
    grid_spec = pltpu.PrefetchScalarGridSpec(
        num_scalar_prefetch=2,
        grid=(n_blocks,),
        in_specs=[
            pl.BlockSpec((bm, D_MODEL), row),
            pl.BlockSpec((bm, 1), row),
            pl.BlockSpec((None, D_MODEL, D_EXPERT), wsel),
            pl.BlockSpec((None, D_MODEL, D_EXPERT), wsel),
            pl.BlockSpec((None, 1, D_EXPERT), wsel),
            pl.BlockSpec((None, 1, D_EXPERT), wsel),
            pl.BlockSpec((None, D_EXPERT, D_MODEL), wsel),
            pl.BlockSpec((None, 1, D_MODEL), wsel),
        ],
        out_specs=pl.BlockSpec((bm, D_MODEL), row),
    )
    return pl.pallas_call(
        _expert_kernel,
        grid_spec=grid_spec,
        out_shape=jax.ShapeDtypeStruct((n_slots, D_MODEL), F32),
        compiler_params=_cparams(("arbitrary",)),
        name="moe_experts",
    )(block_e, n_used, xs, slot_gate, wg, wl, bg, bl, wd, bd)


def _ple_kernel(h_ref, moe_ref, p_ref, wg_ref, wp_ref, fg_ref, o_ref, *, final_norm):
    h = h_ref[...] + moe_ref[...]
    z = jnp.dot(h.astype(BF16), wg_ref[...], preferred_element_type=F32)
    gate = 1.0 / (1.0 + jnp.exp(-z))
    proj = jnp.dot(p_ref[...].astype(BF16), wp_ref[...], preferred_element_type=F32)
    h = h + proj * gate
    if final_norm:
        h = h * lax.rsqrt(jnp.mean(h * h, axis=-1, keepdims=True) + NORM_EPS) * fg_ref[...]
    o_ref[...] = h


def _ple(h2d, moe, p2d, wg_bf16, wp_bf16, final_gain, final_norm):
    t = h2d.shape[0]
    tm = PLE_TM
    row = lambda i: (i, 0)
    const = lambda i: (0, 0)
    return pl.pallas_call(
        functools.partial(_ple_kernel, final_norm=final_norm),
        grid=(t // tm,),
        in_specs=[pl.BlockSpec((tm, D_MODEL), row), pl.BlockSpec((tm, D_MODEL), row), pl.BlockSpec((tm, PLE_DIM), row),
                  pl.BlockSpec((D_MODEL, D_MODEL), const), pl.BlockSpec((PLE_DIM, D_MODEL), const),
                  pl.BlockSpec((1, D_MODEL), const)],
        out_specs=pl.BlockSpec((tm, D_MODEL), row),
        out_shape=jax.ShapeDtypeStruct((t, D_MODEL), F32),
        compiler_params=_cparams(("parallel",)),
        name="ple_gate",
    )(h2d, moe, p2d, wg_bf16, wp_bf16, final_gain.reshape(1, D_MODEL))


def _rope_tables(seq):
    lane = np.arange(LANES)
    pos = jnp.arange(seq, dtype=F32)[:, None]
    f1 = ROPE_THETA ** (-jnp.arange(0, HEAD_DIM, 2, dtype=F32) / HEAD_DIM)
    ang1 = (pos * f1[None, :])[:, lane % (HEAD_DIM // 2)]
    sgn1 = jnp.asarray(np.where(lane % HEAD_DIM < HEAD_DIM // 2, -1.0, 1.0), F32)
    half = HEAD_DIM // 2
    fa = ROPE_THETA ** (-jnp.arange(0, half, 2, dtype=F32) / half)
    rows = jnp.arange(seq) // GRID_W
    cols = jnp.arange(seq) % GRID_W
    ang_r = rows.astype(F32)[:, None] * fa[None, :]
    ang_c = cols.astype(F32)[:, None] * fa[None, :]
    fidx = lane % (half // 2)
    use_row = jnp.asarray(lane % HEAD_DIM < half)
    anga = jnp.where(use_row[None, :], ang_r[:, fidx], ang_c[:, fidx])
    sgna = jnp.asarray(np.where(lane % half < half // 2, -1.0, 1.0), F32)
    return jnp.cos(ang1), jnp.sin(ang1) * sgn1, jnp.cos(anga), jnp.sin(anga) * sgna


def _moe_layer(h2d, gain, router_w, router_b, w_up, b_up, w_down, b_down):
    t = h2d.shape[0]
    rw_pad = jnp.zeros((D_MODEL, LANES), BF16).at[:, :N_EXPERTS].set(router_w.astype(BF16))
    rb_pad = jnp.full((1, LANES), -jnp.inf, F32).at[0, :N_EXPERTS].set(router_b)
    hn, idx, gates = _router(h2d, gain, rw_pad, rb_pad)
    a = t * TOP_K
    bm = MOE_BM
    flat_e = idx[:, :TOP_K].reshape(a)
    flat_gate = gates[:, :TOP_K].reshape(a)
    flat_tok = jnp.repeat(jnp.arange(t, dtype=jnp.int32), TOP_K)
    onehot = (flat_e[:, None] == jnp.arange(N_EXPERTS, dtype=jnp.int32)[None, :]).astype(jnp.int32)
    csum = jnp.cumsum(onehot, axis=0)
    rank = jnp.take_along_axis(csum, flat_e[:, None], axis=1)[:, 0] - 1
    counts = csum[-1]
    padded = (counts + bm - 1) // bm * bm
    pends = jnp.cumsum(padded)
    pstarts = pends - padded
    dest = pstarts[flat_e] + rank
    n_blocks = -(-a // bm) + N_EXPERTS
    n_slots = n_blocks * bm
    slot_tok = jnp.full((n_slots,), t, jnp.int32).at[dest].set(flat_tok)
    slot_gate = jnp.zeros((n_slots,), F32).at[dest].set(flat_gate)
    block_e = jnp.minimum(jnp.searchsorted(pends, jnp.arange(n_blocks, dtype=jnp.int32) * bm, side='right'),
                          N_EXPERTS - 1).astype(jnp.int32)
    n_used = (pends[-1] // bm).astype(jnp.int32).reshape(1)
    hn_pad = jnp.concatenate([hn, jnp.zeros((1, D_MODEL), hn.dtype)], axis=0)
    xs = hn_pad[slot_tok]
    wg = w_up[:, :, 0::2].astype(BF16)
    wl = w_up[:, :, 1::2].astype(BF16)
    bg = b_up[:, None, 0::2]
    bl = b_up[:, None, 1::2]
    ys = _experts(xs, slot_gate[:, None], block_e, n_used, wg, wl, bg, bl, w_down.astype(BF16), b_down[:, None, :])
    return ys[dest].reshape(t, TOP_K, D_MODEL).sum(axis=1)


def kernel(x, p, norm_mix, ab_w_in, ab_q_norm, ab_k_norm, ab_sink, ab_w_out, c_w_in, c_w_out, norm_ffn,
           router_w, router_b, exp_w_up, exp_b_up, exp_w_down, exp_b_down, ple_w_proj, ple_w_gate, norm_final):
    batch, seq, _ = x.shape
    t = batch * seq
    depth = p.shape[0]
    tabs = _rope_tables(seq)
    lane = np.arange(LANES)
    bd = jnp.asarray(lane[:, None] // HEAD_DIM == lane[None, :] // HEAD_DIM, BF16)
    ones = jnp.ones((1, LANES), F32)
    h = x.reshape(t, D_MODEL)
    for i in range(depth):
        j = i // 2
        if i % 2 == 0:
            qg = jnp.tile(ab_q_norm[j], LANES // HEAD_DIM)[None, :]
            kg = jnp.tile(ab_k_norm[j], LANES // HEAD_DIM)[None, :]
            qa, ka4, va4, qb, kb4, vb4 = _project(h, norm_mix[i], ab_w_in[j].astype(BF16), tabs, qg, kg, bd,
                                                  axial_units=(0,), seq=seq)
            oa = _dense_attention(qa, ka4, va4, batch, seq)
            ob, = _banded_attention(qb, kb4, vb4, batch, seq, B_HALF_WINDOW, 1, sink=ab_sink[j])
            h = _out_proj_ab(h, oa, ob, ab_w_out[j].astype(BF16))
        else:
            units = _project(h, norm_mix[i], c_w_in[j].astype(BF16), tabs, ones, ones, bd, axial_units=(), seq=seq)
            outs, lses = [], []
            for gi, (window, dilation) in enumerate(C_PATTERNS):
                q, k4, v4 = units[3 * gi:3 * gi + 3]
                o, lse = _banded_attention(q, k4, v4, batch, seq, window // (2 * dilation), dilation,
                                           want_lse=True, out_dtype=F32)
                outs.append(o)
                lses.append(lse)
            h = _out_proj_c(h, outs, lses, c_w_out[j].astype(BF16))
        moe = _moe_layer(h, norm_ffn[i], router_w[i], router_b[i], exp_w_up[i], exp_b_up[i],
                         exp_w_down[i], exp_b_down[i])
        h = _ple(h, moe, p[i].reshape(t, PLE_DIM), ple_w_gate[i].astype(BF16), ple_w_proj[i].astype(BF16),
                 norm_final, final_norm=(i == depth - 1))
    return h.reshape(batch, seq, D_MODEL)
```
